```python
import math
import jax, jax.numpy as jnp
from jax import lax
import numpy as np

D_MODEL = 1024
BATCH = 4
SEQ = 8192
DEPTH = 1
DEC_BATCH = 128
DEC_SEQ = 1
PAST_LEN = 8192
PAGE_SIZE = 128

SB_HEAD_DIM = 64
SB_HEADS = D_MODEL // 128
SB_WIDTH = SB_HEADS * SB_HEAD_DIM
DIFF_HEAD_DIM = 64
DIFF_HEADS = D_MODEL // 256
DIFF_QK_WIDTH = DIFF_HEADS * 2 * DIFF_HEAD_DIM
DIFF_V_DIM = 2 * DIFF_HEAD_DIM
DIFF_V_WIDTH = DIFF_HEADS * DIFF_V_DIM
D_FF = -(-8 * D_MODEL // (3 * 256)) * 256
Q_BLOCK = 128
RMS_EPS = 1e-6
IN_SPLIT_SIZES = (SB_WIDTH, SB_WIDTH, SB_WIDTH, DIFF_QK_WIDTH, DIFF_QK_WIDTH, DIFF_V_WIDTH, D_MODEL, D_MODEL)
IN_SPLITS = tuple(int(s) for s in np.cumsum(IN_SPLIT_SIZES)[:-1])
IN_WIDTH = int(sum(IN_SPLIT_SIZES))

kernel_name = "stick_breaking_diff_attn_hybrid_step"


def _rmsnorm(x, g):
    xf = x.astype(jnp.float32)
    y = xf * lax.rsqrt(jnp.mean(xf * xf, axis=-1, keepdims=True) + RMS_EPS)
    return y.astype(x.dtype) * g


def _alibi_slopes(n_heads):
    return jnp.exp2(-8.0 * (jnp.arange(n_heads, dtype=jnp.float32) + 1.0) / n_heads)


def _query_blocks(fn, q, q_pos):
    t = q.shape[1]
    if t <= Q_BLOCK or t % Q_BLOCK:
        return fn(q, q_pos)
    nb = t // Q_BLOCK
    b = q.shape[0]
    qb = jnp.moveaxis(q.reshape(b, nb, Q_BLOCK, *q.shape[2:]), 1, 0)
    pb = q_pos.reshape(nb, Q_BLOCK)
    out = lax.map(lambda a: fn(a[0], a[1]), (qb, pb))
    out = jnp.moveaxis(out, 0, 1)
    return out.reshape(b, t, *out.shape[3:])


def _stick_breaking_block(q, q_pos, k, v, k_pos):
    z = jnp.einsum('bqhd,bkhd->bhqk', q, k).astype(jnp.float32) * (1.0 / math.sqrt(SB_HEAD_DIM))
    mask = k_pos[None, :] < q_pos[:, None]
    log_beta = jax.nn.log_sigmoid(z)
    log_one_minus = jnp.where(mask, jax.nn.log_sigmoid(-z), 0.0)
    later = lax.cumsum(log_one_minus, axis=3, reverse=True) - log_one_minus
    w = jnp.where(mask, jnp.exp(log_beta + later), 0.0)
    return jnp.einsum('bhqk,bkhd->bqhd', w.astype(v.dtype), v)


def _diff_block(q, q_pos, k, v, k_pos, slopes, lam):
    z = jnp.einsum('bqhcd,bkhcd->bhcqk', q, k).astype(jnp.float32) * (1.0 / math.sqrt(DIFF_HEAD_DIM))
    dist = (q_pos[:, None] - k_pos[None, :]).astype(jnp.float32)
    z = z - slopes[None, :, None, None, None] * dist
    z = jnp.where(dist >= 0, z, -jnp.inf)
    p = jax.nn.softmax(z, axis=-1)
    w = p[:, :, 0] - lam * p[:, :, 1]
    return jnp.einsum('bhqk,bkhe->bqhe', w.astype(v.dtype), v)


def _mixer(xn, pos, past, past_pos, w_in, lq1, lk1, lq2, lk2, subln_g,
           w_sb_branch, w_diff_branch, w_o, lam_init):
    b, t, _ = xn.shape
    parts = jnp.split(xn @ w_in, IN_SPLITS, axis=-1)
    q_sb, k_sb, v_sb = [p_.reshape(b, t, SB_HEADS, SB_HEAD_DIM) for p_ in parts[:3]]
    q_df, k_df = [p_.reshape(b, t, DIFF_HEADS, 2, DIFF_HEAD_DIM) for p_ in parts[3:5]]
    v_df = parts[5].reshape(b, t, DIFF_HEADS, DIFF_V_DIM)
    g_sb, g_df = parts[6], parts[7]
    new_rows = (k_sb, v_sb, k_df, v_df)
    if past is None:
        keys, k_pos = new_rows, pos
    else:
        keys = tuple(jnp.concatenate([pa, nw], axis=1) for pa, nw in zip(past, new_rows))
        k_pos = jnp.concatenate([past_pos, pos])
    kk_sb, vv_sb, kk_df, vv_df = keys

    y_sb = _query_blocks(lambda qb, pb: _stick_breaking_block(qb, pb, kk_sb, vv_sb, k_pos), q_sb, pos)

    lam = (jnp.exp(jnp.sum(lq1.astype(jnp.float32) * lk1.astype(jnp.float32)))
           - jnp.exp(jnp.sum(lq2.astype(jnp.float32) * lk2.astype(jnp.float32))) + lam_init)
    slopes = _alibi_slopes(DIFF_HEADS)
    y_df = _query_blocks(lambda qb, pb: _diff_block(qb, pb, kk_df, vv_df, k_pos, slopes, lam), q_df, pos)
    y_df = _rmsnorm(y_df, subln_g) * (1.0 - lam_init)

    ya = y_sb.reshape(b, t, SB_WIDTH) @ w_sb_branch
    yb = y_df.reshape(b, t, DIFF_V_WIDTH) @ w_diff_branch
    merged = jax.nn.sigmoid(g_sb) * ya + jax.nn.sigmoid(g_df) * yb
    return merged @ w_o, new_rows


def _swiglu(xn, w_ffn_in, w_ffn_out):
    gate, up = jnp.split(xn @ w_ffn_in, 2, axis=-1)
    return (jax.nn.silu(gate) * up) @ w_ffn_out


def _trunk(x, pos, pasts, past_pos, norm_mix_g, w_in, diff_lambda_q1, diff_lambda_k1,
           diff_lambda_q2, diff_lambda_k2, diff_subln_g, w_sb_branch, w_diff_branch, w_o,
           norm_ffn_g, w_ffn_in, w_ffn_out, norm_final_g):
    rows = ([], [], [], [])
    h = x
    for l in range(DEPTH):
        lam_init = 0.8 - 0.6 * math.exp(-0.3 * l)
        past = None if pasts is None else pasts(l)
        mix, new_rows = _mixer(_rmsnorm(h, norm_mix_g[l]), pos, past, past_pos, w_in[l],
                               diff_lambda_q1[l], diff_lambda_k1[l], diff_lambda_q2[l], diff_lambda_k2[l],
                               diff_subln_g[l], w_sb_branch[l], w_diff_branch[l], w_o[l], lam_init)
        h = h + mix
        h = h + _swiglu(_rmsnorm(h, norm_ffn_g[l]), w_ffn_in[l], w_ffn_out[l])
        for acc, r in zip(rows, new_rows):
            acc.append(r)
    stacked = tuple(jnp.stack(acc, axis=0) for acc in rows)
    return _rmsnorm(h, norm_final_g), stacked


def setup_inputs(seed: int = 0) -> dict:
    key = jax.random.key(seed)
    ks = jax.random.split(key, 24)
    n_pages = PAST_LEN // PAGE_SIZE
    n_phys = (DEC_BATCH * n_pages * 5) // 4
    f32 = jnp.float32
    nrm = lambda k, s, sc=1.0: jax.random.normal(k, s, f32) * sc
    page_table = jax.random.permutation(ks[6], n_phys)[: DEC_BATCH * n_pages].reshape(DEC_BATCH, n_pages).astype(jnp.int32)
    return {
        "x_prompt": nrm(ks[0], (BATCH, SEQ, D_MODEL)),
        "x_sample": nrm(ks[1], (DEC_BATCH, DEC_SEQ, D_MODEL)),
        "cache_sb_k": nrm(ks[2], (DEPTH, n_phys, PAGE_SIZE, SB_HEADS, SB_HEAD_DIM)),
        "cache_sb_v": nrm(ks[3], (DEPTH, n_phys, PAGE_SIZE, SB_HEADS, SB_HEAD_DIM)),
        "cache_diff_k": nrm(ks[4], (DEPTH, n_phys, PAGE_SIZE, DIFF_HEADS, 2, DIFF_HEAD_DIM)),
        "cache_diff_v": nrm(ks[5], (DEPTH, n_phys, PAGE_SIZE, DIFF_HEADS, DIFF_V_DIM)),
        "page_table": page_table,
        "norm_mix_g": 1.0 + nrm(ks[7], (DEPTH, D_MODEL), 0.02),
        "w_in": nrm(ks[8], (DEPTH, D_MODEL, IN_WIDTH), D_MODEL ** -0.5),
        "diff_lambda_q1": nrm(ks[9], (DEPTH, DIFF_HEAD_DIM), 0.1),
        "diff_lambda_k1": nrm(ks[10], (DEPTH, DIFF_HEAD_DIM), 0.1),
        "diff_lambda_q2": nrm(ks[11], (DEPTH, DIFF_HEAD_DIM), 0.1),
        "diff_lambda_k2": nrm(ks[12], (DEPTH, DIFF_HEAD_DIM), 0.1),
        "diff_subln_g": 1.0 + nrm(ks[13], (DEPTH, DIFF_V_DIM), 0.02),
        "w_sb_branch": nrm(ks[14], (DEPTH, SB_WIDTH, D_MODEL), SB_WIDTH ** -0.5),
        "w_diff_branch": nrm(ks[15], (DEPTH, DIFF_V_WIDTH, D_MODEL), DIFF_V_WIDTH ** -0.5),
        "w_o": nrm(ks[16], (DEPTH, D_MODEL, D_MODEL), D_MODEL ** -0.5),
        "norm_ffn_g": 1.0 + nrm(ks[17], (DEPTH, D_MODEL), 0.02),
        "w_ffn_in": nrm(ks[18], (DEPTH, D_MODEL, 2 * D_FF), D_MODEL ** -0.5),
        "w_ffn_out": nrm(ks[19], (DEPTH, D_FF, D_MODEL), D_FF ** -0.5),
        "norm_final_g": 1.0 + nrm(ks[20], (D_MODEL,), 0.02),
    }


def reference(x_prompt, x_sample, cache_sb_k, cache_sb_v, cache_diff_k, cache_diff_v, page_table,
              norm_mix_g, w_in, diff_lambda_q1, diff_lambda_k1, diff_lambda_q2, diff_lambda_k2,
              diff_subln_g, w_sb_branch, w_diff_branch, w_o, norm_ffn_g, w_ffn_in, w_ffn_out,
              norm_final_g):
    weights = (norm_mix_g, w_in, diff_lambda_q1, diff_lambda_k1, diff_lambda_q2, diff_lambda_k2,
               diff_subln_g, w_sb_branch, w_diff_branch, w_o, norm_ffn_g, w_ffn_in, w_ffn_out,
               norm_final_g)
    pos_p = jnp.arange(x_prompt.shape[1], dtype=jnp.int32)
    y_prompt, (sbk_p, sbv_p, dfk_p, dfv_p) = _trunk(x_prompt, pos_p, None, None, *weights)

    n_seq, n_pages = page_table.shape
    past_len = n_pages * cache_sb_k.shape[2]

    def gather(cache_l):
        g = cache_l[page_table]
        return g.reshape(n_seq, past_len, *g.shape[3:])

    def pasts(l):
        return (gather(cache_sb_k[l]), gather(cache_sb_v[l]), gather(cache_diff_k[l]), gather(cache_diff_v[l]))

    past_pos = jnp.arange(past_len, dtype=jnp.int32)
    pos_s = past_len + jnp.arange(x_sample.shape[1], dtype=jnp.int32)
    y_sample, (sbk_s, sbv_s, dfk_s, dfv_s) = _trunk(x_sample, pos_s, pasts, past_pos, *weights)
    return (y_prompt, y_sample, sbk_p, sbv_p, dfk_p, dfv_p, sbk_s, sbv_s, dfk_s, dfv_s)
```

```python
import functools
import math

import jax
import jax.numpy as jnp
from jax import lax
from jax.experimental import pallas as pl
from jax.experimental.pallas import tpu as pltpu

F32 = jnp.float32
BF16 = jnp.bfloat16

RMS_EPS = 1e-6
HEAD_DIM = 64
LANES = 128
QK_SCALE = 1.0 / math.sqrt(HEAD_DIM)
NEG_BIG = -1e30
SB_LOG_ZERO = -104.0
VMEM_LIMIT_BYTES = 56 * 1024 * 1024

PROJ_ROWS = 512
ATT_TQ = 256
ATT_TK = 256
FFN_ROWS = 256
DEC_PAGES = 8


def _cparams(sem):
    return pltpu.CompilerParams(dimension_semantics=sem, vmem_limit_bytes=VMEM_LIMIT_BYTES)


def _const_spec(shape):
    nd = len(shape)
    return pl.BlockSpec(shape, lambda *_: (0,) * nd, pipeline_mode=pl.Buffered(1))


def _softplus_neg_abs(z):
    return jnp.log(1.0 + jnp.exp(-jnp.abs(z)))


def _proj_kernel(x_ref, g_ref, w_ref, qsb_ref, qdf_ref, ktsb_ref, vtsb_ref, ktdf_ref, vdf_ref,
                 ktbsb_ref, ktbdf_ref, vbsb_ref, vbdf_ref, gsb_ref, gdf_ref, *, width, tk):
    x = x_ref[...]
    ms = jnp.mean(x * x, axis=-1, keepdims=True)
    xb = (x * lax.rsqrt(ms + RMS_EPS) * g_ref[...]).astype(BF16)
    tm = x.shape[0]
    w = width

    def mm(col):
        return jnp.dot(xb, w_ref[:, col * w:(col + 1) * w], preferred_element_type=F32)

    def store_kt(k, kt_ref, ktb_ref):
        kt = k.T
        kt_ref[...] = kt
        for j in range(tm // tk):
            blk = kt[:, j * tk:(j + 1) * tk].astype(BF16)
            ktb_ref[:, j] = blk.reshape(w // LANES, LANES, tk)

    qsb_ref[...] = (mm(0) * QK_SCALE).astype(BF16)
    store_kt(mm(1), ktsb_ref, ktbsb_ref)
    v = mm(2)
    vtsb_ref[...] = v.T
    vbsb_ref[...] = v.astype(BF16)
    qdf_ref[...] = (mm(3) * QK_SCALE).astype(BF16)
    store_kt(mm(4), ktdf_ref, ktbdf_ref)
    v = mm(5)
    vdf_ref[...] = v
    vbdf_ref[...] = v.astype(BF16)
    for half in range(2):
        g = mm(6 + half)
        gsb_ref[:, half * w:(half + 1) * w] = (1.0 / (1.0 + jnp.exp(-g))).astype(BF16)
        g = mm(8 + half)
        gdf_ref[:, half * w:(half + 1) * w] = (1.0 / (1.0 + jnp.exp(-g))).astype(BF16)


def _project(x, g, w_in_bf16, *, tm, tk):
    b, t, d = x.shape
    w = d // 2
    assert w_in_bf16.shape == (d, 10 * w) and t % tm == 0 and tm % tk == 0
    nt = t // tm
    nkb = tm // tk
    npair = w // LANES
    row_spec = lambda width: pl.BlockSpec((None, tm, width), lambda i, j: (i, j, 0))
    tr_spec = pl.BlockSpec((None, w, tm), lambda i, j: (i, 0, j))
    ktb_spec = pl.BlockSpec((None, npair, nkb, LANES, tk), lambda i, j: (i, 0, j, 0, 0))
    sds = jax.ShapeDtypeStruct
    out_shape = (
        sds((b, t, w), BF16), sds((b, t, w), BF16),
        sds((b, w, t), F32), sds((b, w, t), F32), sds((b, w, t), F32),
        sds((b, t, w), F32),
        sds((b, npair, t // tk, LANES, tk), BF16), sds((b, npair, t // tk, LANES, tk), BF16),
        sds((b, t, w), BF16), sds((b, t, w), BF16),
        sds((b, t, d), BF16), sds((b, t, d), BF16),
    )
    out_specs = (row_spec(w), row_spec(w), tr_spec, tr_spec, tr_spec, row_spec(w),
                 ktb_spec, ktb_spec, row_spec(w), row_spec(w), row_spec(d), row_spec(d))
    return pl.pallas_call(
        functools.partial(_proj_kernel, width=w, tk=tk),
        grid=(b, nt),
        in_specs=[row_spec(d), _const_spec((1, d)), _const_spec((d, 10 * w))],
        out_specs=out_specs,
        out_shape=out_shape,
        compiler_params=_cparams(("parallel", "parallel")),
        name="proj",
    )(x, g.reshape(1, d), w_in_bf16)


def _sub_head_mask(shape):
    lane = lax.broadcasted_iota(jnp.int32, shape, len(shape) - 1)
    return lane < HEAD_DIM


def _sb_prompt_kernel(q_ref, kt_ref, v_ref, tri_ref, o_ref):
    qi = pl.program_id(2)
    q = q_ref[...]
    tq = q.shape[0]
    tk = tri_ref.shape[0]
    first = _sub_head_mask(q.shape)
    row = lax.broadcasted_iota(jnp.int32, (tq, tk), 0)
    col = lax.broadcasted_iota(jnp.int32, (tq, tk), 1)
    strict = col < row
    tri = tri_ref[...]
    zero = jnp.zeros_like(q)

    def one_head(qh):
        def block(kb, carry, acc, diagonal):
            kt = kt_ref[kb]
            v = v_ref[pl.ds(pl.multiple_of(kb * tk, tk), tk), :]
            z = jnp.dot(qh, kt, preferred_element_type=F32)
            t = _softplus_neg_abs(z)
            log_beta = jnp.minimum(z, 0.0) - t
            log_rest = log_beta - z
            if diagonal:
                log_rest = jnp.where(strict, log_rest, 0.0)
            hi = log_rest.astype(BF16)
            lo = (log_rest - hi.astype(F32)).astype(BF16)
            later = (jnp.dot(hi, tri, preferred_element_type=F32)
                     + jnp.dot(lo, tri, preferred_element_type=F32) + carry)
            wgt = jnp.exp(log_beta + later)
            if diagonal:
                wgt = jnp.where(strict, wgt, 0.0)
            acc = acc + jnp.dot(wgt.astype(BF16), v, preferred_element_type=F32)
            carry = later[:, :1] + log_rest[:, :1]
            return carry, acc

        carry, acc = block(qi, jnp.zeros((tq, 1), F32), jnp.zeros((tq, LANES), F32), True)

        def cond(s):
            kb, carry, _ = s
            return jnp.logical_and(kb >= 0, jnp.max(carry) > SB_LOG_ZERO)

        def body(s):
            kb, carry, acc = s
            carry, acc = block(kb, carry, acc, False)
            return kb - 1, carry, acc

        return lax.while_loop(cond, body, (qi - 1, carry, acc))[2]

    acc0 = one_head(jnp.where(first, q, zero))
    acc1 = one_head(jnp.where(first, zero, q))
    o_ref[...] = jnp.where(first, acc0, acc1).astype(o_ref.dtype)


def _diff_prompt_kernel(slope_ref, lamp_ref, subg_ref, q_ref, kt_ref, v_ref, o_ref, *, lam_init):
    h = pl.program_id(1)
    qi = pl.program_id(2)
    slope = slope_ref[h]
    q = q_ref[...]
    tq = q.shape[0]
    tk = kt_ref.shape[-1]
    first = _sub_head_mask(q.shape)
    zero = jnp.zeros_like(q)
    qs = (jnp.where(first, q, zero), jnp.where(first, zero, q))
    row = lax.broadcasted_iota(jnp.int32, (tq, tk), 0)
    col = lax.broadcasted_iota(jnp.int32, (tq, tk), 1)
    causal = col <= row
    kcol = lax.broadcasted_iota(jnp.int32, (1, tk), 1)

    def step(kb, state, diagonal):
        kt = kt_ref[kb]
        v = v_ref[pl.ds(pl.multiple_of(kb * tk, tk), tk), :]
        bias = slope * (kb * tk - qi * tq + kcol).astype(F32)
        new = []
        for c in range(2):
            m, l, acc = state[c]
            z = jnp.dot(qs[c], kt, preferred_element_type=F32) + bias
            if diagonal:
                z = jnp.where(causal, z, NEG_BIG)
            m_new = jnp.maximum(m, jnp.max(z, axis=1, keepdims=True))
            alpha = jnp.exp(m - m_new)
            p = jnp.exp(z - m_new)
            l = alpha * l + jnp.sum(p, axis=1, keepdims=True)
            acc = alpha * acc + jnp.dot(p.astype(BF16), v, preferred_element_type=F32)
            new.append((m_new, l, acc))
        return tuple(new)

    init = tuple((jnp.full((tq, 1), NEG_BIG, F32), jnp.zeros((tq, 1), F32), jnp.zeros((tq, LANES), F32))
                 for _ in range(2))
    state = lax.fori_loop(0, qi, lambda kb, s: step(kb, s, False), init)
    (_, l1, a1), (_, l2, a2) = step(qi, state, True)

    lp = lamp_ref[...]
    lam = (jnp.exp(jnp.sum(lp[0:1] * lp[1:2], axis=1, keepdims=True))
           - jnp.exp(jnp.sum(lp[2:3] * lp[3:4], axis=1, keepdims=True)) + lam_init)
    y = a1 / l1 - lam * (a2 / l2)
    y = y * lax.rsqrt(jnp.mean(y * y, axis=1, keepdims=True) + RMS_EPS)
    o_ref[...] = (y * subg_ref[...] * (1.0 - lam_init)).astype(o_ref.dtype)


def _prompt_attention(q_sb, ktb_sb, v_sb, q_df, ktb_df, v_df, slopes, lam_params, subln_g, *, lam_init, tq):
    b, t, w = q_sb.shape
    npair, nkb, _, tk = ktb_sb.shape[1:]
    assert tq == tk and t % tq == 0
    grid = (b, npair, t // tq)
    q_spec = pl.BlockSpec((None, tq, LANES), lambda i, p, j: (i, j, p))
    kt_spec = pl.BlockSpec((None, None, nkb, LANES, tk), lambda i, p, j: (i, p, 0, 0, 0))
    v_spec = pl.BlockSpec((None, t, LANES), lambda i, p, j: (i, 0, p))
    out_shape = jax.ShapeDtypeStruct((b, t, w), BF16)
    tri = (lax.broadcasted_iota(jnp.int32, (tk, tk), 0) > lax.broadcasted_iota(jnp.int32, (tk, tk), 1)).astype(BF16)
    y_sb = pl.pallas_call(
        _sb_prompt_kernel, grid=grid,
        in_specs=[q_spec, kt_spec, v_spec, _const_spec((tk, tk))],
        out_specs=q_spec, out_shape=out_shape,
        compiler_params=_cparams(("parallel", "parallel", "arbitrary")),
        name="sb_prompt",
    )(q_sb, ktb_sb, v_sb, tri)
    y_df = pl.pallas_call(
        functools.partial(_diff_prompt_kernel, lam_init=lam_init), grid=grid,
        in_specs=[pl.BlockSpec(memory_space=pltpu.SMEM), _const_spec((4, HEAD_DIM)), _const_spec((1, LANES)),
                  q_spec, kt_spec, v_spec],
        out_specs=q_spec, out_shape=out_shape,
        compiler_params=_cparams(("parallel", "parallel", "arbitrary")),
        name="diff_prompt",
    )(slopes, lam_params, subln_g, q_df, ktb_df, v_df)
    return y_sb, y_df


def _out_ffn_kernel(x_ref, ysb_ref, ydf_ref, gsb_ref, gdf_ref, wsb_ref, wdf_ref, wo_ref, gffn_ref,
                    wfi_ref, wfo_ref, gfin_ref, o_ref):
    ya = jnp.dot(ysb_ref[...], wsb_ref[...], preferred_element_type=F32)
    yb = jnp.dot(ydf_ref[...], wdf_ref[...], preferred_element_type=F32)
    merged = gsb_ref[...].astype(F32) * ya + gdf_ref[...].astype(F32) * yb
    h = x_ref[...] + jnp.dot(merged.astype(BF16), wo_ref[...], preferred_element_type=F32)
    hn = h * lax.rsqrt(jnp.mean(h * h, axis=-1, keepdims=True) + RMS_EPS) * gffn_ref[...]
    up = jnp.dot(hn.astype(BF16), wfi_ref[...], preferred_element_type=F32)
    dff = wfo_ref.shape[0]
    gate = up[:, :dff]
    act = (gate / (1.0 + jnp.exp(-gate))) * up[:, dff:]
    h = h + jnp.dot(act.astype(BF16), wfo_ref[...], preferred_element_type=F32)
    o_ref[...] = h * lax.rsqrt(jnp.mean(h * h, axis=-1, keepdims=True) + RMS_EPS) * gfin_ref[...]


def _out_ffn(x, y_sb, y_df, g_sb, g_df, w_sb, w_df, w_o, g_ffn, w_fi, w_fo, g_fin, *, tm):
    r, d = x.shape
    w = y_sb.shape[1]
    dff = w_fo.shape[0]
    assert r % tm == 0
    row = lambda width: pl.BlockSpec((tm, width), lambda i: (i, 0))
    return pl.pallas_call(
        _out_ffn_kernel, grid=(r // tm,),
        in_specs=[row(d), row(w), row(w), row(d), row(d),
                  _const_spec((w, d)), _const_spec((w, d)), _const_spec((d, d)), _const_spec((1, d)),
                  _const_spec((d, 2 * dff)), _const_spec((dff, d)), _const_spec((1, d))],
        out_specs=row(d), out_shape=jax.ShapeDtypeStruct((r, d), F32),
        compiler_params=_cparams(("parallel",)),
        name="out_ffn",
    )(x, y_sb, y_df, g_sb, g_df, w_sb, w_df, w_o, g_ffn.reshape(1, d), w_fi, w_fo, g_fin.reshape(1, d))


def _block_diag_q(q_row, nrows):
    shape = (nrows, q_row.shape[1])
    lane = lax.broadcasted_iota(jnp.int32, shape, 1)
    r = lax.broadcasted_iota(jnp.int32, shape, 0)
    qb = jnp.broadcast_to(q_row.astype(F32), shape)
    return jnp.where(lane // HEAD_DIM == r, qb, 0.0).astype(q_row.dtype)


def _sb_decode_kernel(pt_ref, q_ref, *refs, pages):
    del pt_ref
    k_refs = refs[:pages]
    v_refs = refs[pages:2 * pages]
    tri_ref, o_ref, carry_ref, acc_ref = refs[2 * pages:]
    ci = pl.program_id(1)
    nh = acc_ref.shape[0]

    @pl.when(ci == 0)
    def _():
        carry_ref[...] = jnp.zeros_like(carry_ref)
        acc_ref[...] = jnp.zeros_like(acc_ref)

    @pl.when(jnp.max(carry_ref[...]) > SB_LOG_ZERO)
    def _():
        qbd = _block_diag_q(q_ref[...], nh)
        tri = tri_ref[...]
        carry = carry_ref[...]
        for j in reversed(range(pages)):
            kt = k_refs[j][...].reshape(nh * HEAD_DIM, LANES).astype(BF16)
            z = jnp.dot(qbd, kt, preferred_element_type=F32)
            t = _softplus_neg_abs(z)
            log_beta = jnp.minimum(z, 0.0) - t
            log_rest = log_beta - z
            hi = log_rest.astype(BF16)
            lo = (log_rest - hi.astype(F32)).astype(BF16)
            later = (jnp.dot(hi, tri, preferred_element_type=F32)
                     + jnp.dot(lo, tri, preferred_element_type=F32) + carry)
            wgt = jnp.exp(log_beta + later)
            for hh in range(nh):
                acc_ref[hh] += v_refs[j][hh] * wgt[hh:hh + 1, :]
            carry = carry + jnp.sum(log_rest, axis=1, keepdims=True)
        carry_ref[...] = carry

    @pl.when(ci == pl.num_programs(1) - 1)
    def _():
        for p in range(nh // 2):
            pair = acc_ref[2 * p:2 * p + 2].reshape(2 * HEAD_DIM, LANES)
            o_ref[p:p + 1, :] = jnp.sum(pair.T, axis=0, keepdims=True)


def _diff_decode_kernel(pt_ref, q_ref, knew_ref, vnew_ref, lamp_ref, subg_ref, expand_ref, *refs,
                        pages, past_len, lam_init):
    del pt_ref
    k_refs = refs[:pages]
    v_refs = refs[pages:2 * pages]
    o_ref, m_ref, l_ref, acc_ref = refs[2 * pages:]
    ci = pl.program_id(1)
    nr = acc_ref.shape[0]
    nheads = nr // 2

    @pl.when(ci == 0)
    def _():
        m_ref[...] = jnp.full_like(m_ref, NEG_BIG)
        l_ref[...] = jnp.zeros_like(l_ref)
        acc_ref[...] = jnp.zeros_like(acc_ref)

    qbd = _block_diag_q(q_ref[...], nr)
    rowi = lax.broadcasted_iota(jnp.int32, (nr, LANES), 0)
    lanei = lax.broadcasted_iota(jnp.int32, (nr, LANES), 1)
    slope = jnp.exp2(-8.0 * ((rowi // 2).astype(F32) + 1.0) / nheads)
    zs = []
    for j in range(pages):
        kt = k_refs[j][...].reshape(nr * HEAD_DIM, LANES).astype(BF16)
        z = jnp.dot(qbd, kt, preferred_element_type=F32)
        kpos = (ci * pages + j) * LANES + lanei
        zs.append(z - slope * (past_len - kpos).astype(F32))
    m = m_ref[...]
    m_new = m
    for z in zs:
        m_new = jnp.maximum(m_new, jnp.max(z, axis=1, keepdims=True))
    alpha = jnp.exp(m - m_new)
    l = alpha * l_ref[...]
    acc = alpha * acc_ref[...]
    wide = (nr, LANES * nheads)
    own = (lax.broadcasted_iota(jnp.int32, wide, 1) % nheads) == (lax.broadcasted_iota(jnp.int32, wide, 0) // 2)
    for j in range(pages):
        p = jnp.exp(zs[j] - m_new)
        l = l + jnp.sum(p, axis=1, keepdims=True)
        pw = jnp.dot(p.astype(BF16), expand_ref[...], preferred_element_type=F32)
        pw = jnp.where(own, pw, 0.0).astype(BF16)
        acc = acc + jnp.dot(pw, v_refs[j][...].astype(BF16), preferred_element_type=F32)
    m_ref[...] = m_new
    l_ref[...] = l
    acc_ref[...] = acc

    @pl.when(ci == pl.num_programs(1) - 1)
    def _():
        z_self = jnp.sum(qbd.astype(F32) * knew_ref[...], axis=1, keepdims=True)
        m_fin = jnp.maximum(m_new, z_self)
        a = jnp.exp(m_new - m_fin)
        p_self = jnp.exp(z_self - m_fin)
        l_fin = a * l + p_self
        acc_fin = a * acc + p_self * vnew_ref[...]
        lp = lamp_ref[...]
        lam = (jnp.exp(jnp.sum(lp[0:1] * lp[1:2], axis=1, keepdims=True))
               - jnp.exp(jnp.sum(lp[2:3] * lp[3:4], axis=1, keepdims=True)) + lam_init)
        odd = (lax.broadcasted_iota(jnp.int32, (nr, 1), 0) % 2) == 1
        t = acc_fin / l_fin * jnp.where(odd, -lam, 1.0)
        y = t + pltpu.roll(t, nr - 1, 0)
        y = y * lax.rsqrt(jnp.mean(y * y, axis=1, keepdims=True) + RMS_EPS)
        o_ref[...] = y * subg_ref[...] * (1.0 - lam_init)


def _decode_attention(page_table, ck_sb, cv_sb, ck_df, cv_df, q_sb, q_df, knew_df, vnew_df,
                      lam_params, subln_g, *, lam_init, pages):
    s, npg = page_table.shape
    nh = ck_sb.shape[1]
    w = q_sb.shape[-1]
    assert npg % pages == 0
    nchunk = npg // pages
    pt = page_table.reshape(-1)
    past_len = npg * LANES

    def page_spec(block, j, reverse):
        nd = len(block)
        def imap(i, c, pt_ref):
            chunk = (nchunk - 1 - c) if reverse else c
            return (pt_ref[i * npg + chunk * pages + j],) + (0,) * nd
        return pl.BlockSpec((None,) + block, imap)

    seq_spec = lambda block: pl.BlockSpec((None,) + block, lambda i, c, pt_ref: (i,) + (0,) * len(block))
    const = lambda shape: pl.BlockSpec(shape, lambda i, c, pt_ref: (0,) * len(shape))

    tri = (lax.broadcasted_iota(jnp.int32, (LANES, LANES), 0)
           > lax.broadcasted_iota(jnp.int32, (LANES, LANES), 1)).astype(BF16)
    kblock = (nh, HEAD_DIM, LANES)
    y_sb = pl.pallas_call(
        functools.partial(_sb_decode_kernel, pages=pages),
        grid_spec=pltpu.PrefetchScalarGridSpec(
            num_scalar_prefetch=1, grid=(s, nchunk),
            in_specs=[seq_spec((1, w))]
                     + [page_spec(kblock, j, True) for j in range(pages)]
                     + [page_spec(kblock, j, True) for j in range(pages)]
                     + [const((LANES, LANES))],
            out_specs=seq_spec((nh // 2, LANES)),
            scratch_shapes=[pltpu.VMEM((nh, 1), F32), pltpu.VMEM((nh, HEAD_DIM, LANES), F32)]),
        out_shape=jax.ShapeDtypeStruct((s, nh // 2, LANES), F32),
        compiler_params=_cparams(("parallel", "arbitrary")),
        name="sb_decode",
    )(pt, q_sb, *([ck_sb] * pages), *([cv_sb] * pages), tri)

    nheads = cv_df.shape[1] // LANES
    expand = (lax.broadcasted_iota(jnp.int32, (LANES, LANES * nheads), 1) // nheads
              == lax.broadcasted_iota(jnp.int32, (LANES, LANES * nheads), 0)).astype(BF16)
    vblock = (LANES * nheads, LANES)
    y_df = pl.pallas_call(
        functools.partial(_diff_decode_kernel, pages=pages, past_len=past_len, lam_init=lam_init),
        grid_spec=pltpu.PrefetchScalarGridSpec(
            num_scalar_prefetch=1, grid=(s, nchunk),
            in_specs=[seq_spec((1, w)), seq_spec((1, w)), seq_spec((2 * nheads, LANES)),
                      const((4, HEAD_DIM)), const((1, LANES)), const((LANES, LANES * nheads))]
                     + [page_spec(kblock, j, False) for j in range(pages)]
                     + [page_spec(vblock, j, False) for j in range(pages)],
            out_specs=seq_spec((2 * nheads, LANES)),
            scratch_shapes=[pltpu.VMEM((2 * nheads, 1), F32), pltpu.VMEM((2 * nheads, 1), F32),
                            pltpu.VMEM((2 * nheads, LANES), F32)]),
        out_shape=jax.ShapeDtypeStruct((s, 2 * nheads, LANES), F32),
        compiler_params=_cparams(("parallel", "arbitrary")),
        name="diff_decode",
    )(pt, q_df, knew_df, vnew_df, lam_params, subln_g, expand, *([ck_df] * pages), *([cv_df] * pages))
    return y_sb.reshape(s, w), y_df[:, ::2, :].reshape(s, w)


def _heads_last(xt, head_shape):
    b, _, t = xt.shape
    nd = len(head_shape)
    xt = xt.reshape((b,) + head_shape + (HEAD_DIM, t))
    return jnp.transpose(xt, (0, nd + 2) + tuple(range(1, nd + 2)))


def kernel(x_prompt, x_sample, cache_sb_k, cache_sb_v, cache_diff_k, cache_diff_v, page_table, norm_mix_g, w_in, diff_lambda_q1, diff_lambda_k1, diff_lambda_q2, diff_lambda_k2, diff_subln_g, w_sb_branch, w_diff_branch, w_o, norm_ffn_g, w_ffn_in, w_ffn_out, norm_final_g):
    depth = w_in.shape[0]
    assert depth == 1, "single-layer step"
    b, t, d = x_prompt.shape
    s = x_sample.shape[0]
    assert x_sample.shape[1] == 1
    w = d // 2
    sb_heads = w // HEAD_DIM
    df_heads = w // LANES
    lam_init = 0.8 - 0.6 * math.exp(-0.3 * 0)

    w_in_b = w_in[0].astype(BF16)
    w_sb_b = w_sb_branch[0].astype(BF16)
    w_df_b = w_diff_branch[0].astype(BF16)
    w_o_b = w_o[0].astype(BF16)
    w_fi_b = w_ffn_in[0].astype(BF16)
    w_fo_b = w_ffn_out[0].astype(BF16)
    lam_params = jnp.concatenate([diff_lambda_q1, diff_lambda_k1, diff_lambda_q2, diff_lambda_k2], axis=0)
    subg = diff_subln_g.reshape(1, LANES)
    slopes = jnp.exp2(-8.0 * (jnp.arange(df_heads, dtype=F32) + 1.0) / df_heads)
    tail = (w_sb_b, w_df_b, w_o_b, norm_ffn_g[0], w_fi_b, w_fo_b, norm_final_g)

    (q_sb, q_df, kt_sb, vt_sb, kt_df, v_df, ktb_sb, ktb_df, vb_sb, vb_df, g_sb, g_df) = _project(
        x_prompt, norm_mix_g[0], w_in_b, tm=PROJ_ROWS, tk=ATT_TK)
    y_sb, y_df = _prompt_attention(q_sb, ktb_sb, vb_sb, q_df, ktb_df, vb_df, slopes, lam_params, subg,
                                   lam_init=lam_init, tq=ATT_TQ)
    r = b * t
    y_prompt = _out_ffn(x_prompt.reshape(r, d), y_sb.reshape(r, w), y_df.reshape(r, w),
                        g_sb.reshape(r, d), g_df.reshape(r, d), *tail, tm=FFN_ROWS).reshape(b, t, d)
    sbk_p = _heads_last(kt_sb, (sb_heads,))[None]
    sbv_p = _heads_last(vt_sb, (sb_heads,))[None]
    dfk_p = _heads_last(kt_df, (df_heads, 2))[None]
    dfv_p = v_df.reshape(1, b, t, df_heads, LANES)

    (q_sb, q_df, kt_sb, vt_sb, kt_df, v_df, _, _, _, _, g_sb, g_df) = _project(
        x_sample.reshape(1, s, d), norm_mix_g[0], w_in_b, tm=s, tk=s)
    n_phys = cache_sb_k.shape[1]
    ck_sb = jnp.transpose(cache_sb_k[0], (0, 2, 3, 1))
    cv_sb = jnp.transpose(cache_sb_v[0], (0, 2, 3, 1))
    ck_df = jnp.transpose(cache_diff_k[0], (0, 2, 3, 4, 1)).reshape(n_phys, 2 * df_heads, HEAD_DIM, LANES)
    cv_df = cache_diff_v[0].reshape(n_phys, LANES * df_heads, LANES)
    knew_df = jnp.transpose(kt_df[0]).reshape(s, 1, w)
    vnew_df = jnp.repeat(v_df[0].reshape(s, df_heads, LANES), 2, axis=1)
    ys_sb, ys_df = _decode_attention(page_table, ck_sb, cv_sb, ck_df, cv_df,
                                     q_sb.reshape(s, 1, w), q_df.reshape(s, 1, w), knew_df, vnew_df,
                                     lam_params, subg, lam_init=lam_init, pages=DEC_PAGES)
    y_sample = _out_ffn(x_sample.reshape(s, d), ys_sb.astype(BF16), ys_df.astype(BF16),
                        g_sb.reshape(s, d), g_df.reshape(s, d), *tail, tm=s).reshape(s, 1, d)
    sbk_s = _heads_last(kt_sb, (sb_heads,))[None].reshape(1, s, 1, sb_heads, HEAD_DIM)
    sbv_s = _heads_last(vt_sb, (sb_heads,))[None].reshape(1, s, 1, sb_heads, HEAD_DIM)
    dfk_s = _heads_last(kt_df, (df_heads, 2))[None].reshape(1, s, 1, df_heads, 2, HEAD_DIM)
    dfv_s = v_df.reshape(1, s, 1, df_heads, LANES)
    return (y_prompt, y_sample, sbk_p, sbv_p, dfk_p, dfv_p, sbk_s, sbv_s, dfk_s, dfv_s)
```
